```python
import jax
import jax.numpy as jnp
from jax import lax
import numpy as np

D_MODEL = 1024
BATCH = 4
SEQ = 4096
DEPTH = 2
DEC_BATCH = 128
DEC_SEQ = 1
PAST_LEN = 2048
PAGE_SIZE = 128

HEAD_DIM = 64
N_HEADS = D_MODEL // HEAD_DIM
CONV_W = 31
D_FF = 128 * ((8 * D_MODEL // 3 + 127) // 128)
N_EXPERTS = 8
TOP_K = 2
D_EXPERT = 7 * D_MODEL // 2
PLE_DIM = 256
Q_BLOCK = 128
N_A_LAYERS = DEPTH // 2
N_B_LAYERS = DEPTH - N_A_LAYERS
N_DENSE = (DEPTH + 1) // 2
N_MOE = DEPTH // 2
SB_BIAS_LO = 4.0
SB_BIAS_HI = 10.0
EPS = 1e-6

kernel_name = 'yoco_conformer_stickbreak_decoder_step'


def rms_norm(x, g):
    xf = x.astype(jnp.float32)
    y = xf * lax.rsqrt(jnp.mean(xf * xf, axis=-1, keepdims=True) + EPS)
    return (y * g.astype(jnp.float32)).astype(x.dtype)


def layer_norm(x, g, b):
    xf = x.astype(jnp.float32)
    mu = jnp.mean(xf, axis=-1, keepdims=True)
    xc = xf - mu
    y = xc * lax.rsqrt(jnp.mean(xc * xc, axis=-1, keepdims=True) + EPS)
    return (y * g.astype(jnp.float32) + b.astype(jnp.float32)).astype(x.dtype)


def split_heads(t):
    return t.reshape(t.shape[:-1] + (N_HEADS, HEAD_DIM))


def swiglu(h, w_gate, w_up, w_down):
    return (jax.nn.silu(h @ w_gate) * (h @ w_up)) @ w_down


def conv_module(h, state, w_pw1, w_dw, b_dw, ln_g, ln_b, w_pw2):
    a, b = jnp.split(h @ w_pw1, 2, axis=-1)
    g = a * jax.nn.sigmoid(b)
    ext = jnp.concatenate([state.astype(g.dtype), g], axis=1)
    c = lax.conv_general_dilated(ext, w_dw[:, None, :].astype(ext.dtype), window_strides=(1,),
                                 padding='VALID', dimension_numbers=('NWC', 'WIO', 'NWC'),
                                 feature_group_count=D_MODEL)
    c = jax.nn.silu(layer_norm(c + b_dw.astype(c.dtype), ln_g, ln_b))
    return c @ w_pw2, ext[:, ext.shape[1] - (CONV_W - 1):]


def moe(h, w_router, w_g, w_u, w_d):
    logits = jnp.einsum('btd,de->bte', h, w_router).astype(jnp.float32)
    top_v, top_i = lax.top_k(logits, TOP_K)
    gates = jax.nn.softmax(top_v, axis=-1)
    combine = jnp.einsum('btk,btke->bte', gates,
                         jax.nn.one_hot(top_i, N_EXPERTS, dtype=jnp.float32))
    out = jnp.zeros(h.shape, jnp.float32)
    for e in range(N_EXPERTS):
        out = out + combine[..., e:e + 1] * swiglu(h, w_g[e], w_u[e], w_d[e]).astype(jnp.float32)
    return out.astype(h.dtype)


def stick_breaking(q, k, v, bias, q_pos, k_pos):
    z = jnp.einsum('bqhd,bkhd->bhqk', q, k, preferred_element_type=jnp.float32) * (HEAD_DIM ** -0.5)
    z = z + bias.astype(jnp.float32)[None, :, None, None]
    mask = k_pos[None, :] < q_pos[:, None]
    log_1m = jnp.where(mask, jax.nn.log_sigmoid(-z), 0.0)
    suffix = lax.cumsum(log_1m, axis=3, reverse=True) - log_1m
    w = jnp.where(mask, jnp.exp(jax.nn.log_sigmoid(z) + suffix), 0.0)
    o = jnp.einsum('bhqk,bkhd->bqhd', w, v.astype(jnp.float32))
    return o.astype(q.dtype)


def stick_breaking_prompt(q, k, v, bias):
    B, T = q.shape[0], q.shape[1]
    nb = T // Q_BLOCK
    qb = q.reshape(B, nb, Q_BLOCK, N_HEADS, HEAD_DIM).transpose(1, 0, 2, 3, 4)
    k_pos = jnp.arange(T, dtype=jnp.int32)

    def block(args):
        q_i, i = args
        return stick_breaking(q_i, k, v, bias, i * Q_BLOCK + jnp.arange(Q_BLOCK, dtype=jnp.int32), k_pos)

    o = lax.map(block, (qb, jnp.arange(nb, dtype=jnp.int32)))
    return o.transpose(1, 0, 2, 3, 4).reshape(B, T, N_HEADS, HEAD_DIM)


def trunk(x, p, conv_state, past_k, past_v, W):
    new_conv = []
    k_new = v_new = k_all = v_all = None
    for layer in range(DEPTH):
        h = rms_norm(x, W['g_mix'][layer])
        if layer < N_A_LAYERS:
            a = layer
            out, st = conv_module(h, conv_state[a], W['w_pw1'][a], W['w_dw'][a], W['b_dw'][a],
                                  W['ln_g'][a], W['ln_b'][a], W['w_pw2'][a])
            new_conv.append(st)
        else:
            b = layer - N_A_LAYERS
            q = rms_norm(split_heads(h @ W['w_q'][b]), W['g_q'][b])
            if past_k is None:
                o = stick_breaking_prompt(q, k_new, v_new, W['b_sb'][b])
            else:
                T = q.shape[1]
                P = past_k.shape[1]
                o = stick_breaking(q, k_all, v_all, W['b_sb'][b], P + jnp.arange(T, dtype=jnp.int32),
                                   jnp.arange(P + T, dtype=jnp.int32))
            out = o.reshape(o.shape[0], o.shape[1], N_HEADS * HEAD_DIM) @ W['w_o'][b]
        x = x + out.astype(x.dtype)
        h = rms_norm(x, W['g_ffn'][layer])
        i = layer // 2
        if layer % 2 == 0:
            f = swiglu(h, W['w_ff_gate'][i], W['w_ff_up'][i], W['w_ff_down'][i])
        else:
            f = moe(h, W['w_router'][i], W['w_ex_gate'][i], W['w_ex_up'][i], W['w_ex_down'][i])
        x = x + f.astype(x.dtype)
        gate = jax.nn.sigmoid(rms_norm(x, W['g_pe'][layer]) @ W['w_pe_gate'][layer])
        x = x + (gate * (p[layer] @ W['w_pe_proj'][layer])).astype(x.dtype)
        if layer == N_A_LAYERS - 1:
            hkv = rms_norm(x, W['g_kv'])
            k_new = rms_norm(split_heads(hkv @ W['w_k']), W['g_k'])
            v_new = split_heads(hkv @ W['w_v'])
            if past_k is not None:
                k_all = jnp.concatenate([past_k.astype(k_new.dtype), k_new], axis=1)
                v_all = jnp.concatenate([past_v.astype(v_new.dtype), v_new], axis=1)
    return x, jnp.stack(new_conv), k_new, v_new


def setup_inputs(seed: int = 0) -> dict:
    key = jax.random.key(seed)
    ks = iter(jax.random.split(key, 48))
    f32 = jnp.float32

    def nrm(shape, scale):
        return jax.random.normal(next(ks), shape, f32) * scale

    def gain(shape):
        return 1.0 + 0.01 * jax.random.normal(next(ks), shape, f32)

    n_pages = PAST_LEN // PAGE_SIZE
    n_used = DEC_BATCH * n_pages
    n_pool = n_used + max(1, n_used // 4)
    HD = N_HEADS * HEAD_DIM
    page_table = jax.random.permutation(next(ks), n_pool)[:n_used].reshape(DEC_BATCH, n_pages).astype(jnp.int32)
    b_sb = (-jnp.linspace(SB_BIAS_LO, SB_BIAS_HI, N_HEADS, dtype=f32))[None, :] + nrm((N_B_LAYERS, N_HEADS), 0.01)
    return {
        'x_prompt': nrm((BATCH, SEQ, D_MODEL), 1.0),
        'x_sample': nrm((DEC_BATCH, DEC_SEQ, D_MODEL), 1.0),
        'state_conv': nrm((N_A_LAYERS, DEC_BATCH, CONV_W - 1, D_MODEL), 0.5),
        'cache_k': nrm((n_pool, PAGE_SIZE, N_HEADS, HEAD_DIM), 1.0),
        'cache_v': nrm((n_pool, PAGE_SIZE, N_HEADS, HEAD_DIM), 1.0),
        'page_table': page_table,
        'p_prompt': nrm((DEPTH, BATCH, SEQ, PLE_DIM), 1.0),
        'p_sample': nrm((DEPTH, DEC_BATCH, DEC_SEQ, PLE_DIM), 1.0),
        'g_mix': gain((DEPTH, D_MODEL)),
        'w_pw1': nrm((N_A_LAYERS, D_MODEL, 2 * D_MODEL), D_MODEL ** -0.5),
        'w_dw': nrm((N_A_LAYERS, CONV_W, D_MODEL), CONV_W ** -0.5),
        'b_dw': nrm((N_A_LAYERS, D_MODEL), 0.01),
        'ln_g': gain((N_A_LAYERS, D_MODEL)),
        'ln_b': nrm((N_A_LAYERS, D_MODEL), 0.01),
        'w_pw2': nrm((N_A_LAYERS, D_MODEL, D_MODEL), D_MODEL ** -0.5),
        'g_kv': gain((D_MODEL,)),
        'w_k': nrm((D_MODEL, HD), D_MODEL ** -0.5),
        'w_v': nrm((D_MODEL, HD), D_MODEL ** -0.5),
        'g_k': gain((HEAD_DIM,)),
        'w_q': nrm((N_B_LAYERS, D_MODEL, HD), D_MODEL ** -0.5),
        'g_q': gain((N_B_LAYERS, HEAD_DIM)),
        'b_sb': b_sb,
        'w_o': nrm((N_B_LAYERS, HD, D_MODEL), HD ** -0.5),
        'g_ffn': gain((DEPTH, D_MODEL)),
        'w_ff_gate': nrm((N_DENSE, D_MODEL, D_FF), D_MODEL ** -0.5),
        'w_ff_up': nrm((N_DENSE, D_MODEL, D_FF), D_MODEL ** -0.5),
        'w_ff_down': nrm((N_DENSE, D_FF, D_MODEL), D_FF ** -0.5),
        'w_router': nrm((N_MOE, D_MODEL, N_EXPERTS), D_MODEL ** -0.5),
        'w_ex_gate': nrm((N_MOE, N_EXPERTS, D_MODEL, D_EXPERT), D_MODEL ** -0.5),
        'w_ex_up': nrm((N_MOE, N_EXPERTS, D_MODEL, D_EXPERT), D_MODEL ** -0.5),
        'w_ex_down': nrm((N_MOE, N_EXPERTS, D_EXPERT, D_MODEL), D_EXPERT ** -0.5),
        'g_pe': gain((DEPTH, D_MODEL)),
        'w_pe_gate': nrm((DEPTH, D_MODEL, D_MODEL), D_MODEL ** -0.5),
        'w_pe_proj': nrm((DEPTH, PLE_DIM, D_MODEL), PLE_DIM ** -0.5),
    }


def reference(x_prompt, x_sample, state_conv, cache_k, cache_v, page_table, p_prompt, p_sample,
              g_mix, w_pw1, w_dw, b_dw, ln_g, ln_b, w_pw2, g_kv, w_k, w_v, g_k, w_q, g_q, b_sb, w_o,
              g_ffn, w_ff_gate, w_ff_up, w_ff_down, w_router, w_ex_gate, w_ex_up, w_ex_down,
              g_pe, w_pe_gate, w_pe_proj):
    W = dict(g_mix=g_mix, w_pw1=w_pw1, w_dw=w_dw, b_dw=b_dw, ln_g=ln_g, ln_b=ln_b, w_pw2=w_pw2,
             g_kv=g_kv, w_k=w_k, w_v=w_v, g_k=g_k, w_q=w_q, g_q=g_q, b_sb=b_sb, w_o=w_o, g_ffn=g_ffn,
             w_ff_gate=w_ff_gate, w_ff_up=w_ff_up, w_ff_down=w_ff_down, w_router=w_router,
             w_ex_gate=w_ex_gate, w_ex_up=w_ex_up, w_ex_down=w_ex_down, g_pe=g_pe,
             w_pe_gate=w_pe_gate, w_pe_proj=w_pe_proj)
    zero_conv = jnp.zeros((N_A_LAYERS, x_prompt.shape[0], CONV_W - 1, D_MODEL), x_prompt.dtype)
    y_prompt, conv_p, k_p, v_p = trunk(x_prompt, p_prompt, zero_conv, None, None, W)
    db, n_pages = page_table.shape
    past_k = cache_k[page_table].reshape(db, n_pages * PAGE_SIZE, N_HEADS, HEAD_DIM)
    past_v = cache_v[page_table].reshape(db, n_pages * PAGE_SIZE, N_HEADS, HEAD_DIM)
    y_sample, conv_s, k_s, v_s = trunk(x_sample, p_sample, state_conv, past_k, past_v, W)
    return (y_prompt, y_sample, conv_p, conv_s, k_p, v_p, k_s, v_s)
```

```python
import functools

import jax
import jax.numpy as jnp
from jax import lax
from jax.experimental import pallas as pl
from jax.experimental.pallas import tpu as pltpu

F32 = jnp.float32
BF16 = jnp.bfloat16
EPS = 1e-6
HEAD_DIM = 64
N_HEADS = 16
CONV_W = 31
N_EXPERTS = 8
PAGE_SIZE = 128
LANES = 128
VMEM_LIMIT_BYTES = 56 * 1024 * 1024
CHUNK = 128


def _sigmoid(x):
    return 1.0 / (1.0 + jnp.exp(-x))


def _pick_tile(n, cap):
    best = None
    t = LANES
    while t <= min(n, cap):
        if n % t == 0:
            best = t
        t += LANES
    assert best is not None, n
    return best


def _params(sem):
    return pltpu.CompilerParams(dimension_semantics=sem, vmem_limit_bytes=VMEM_LIMIT_BYTES)


def _mm_body(*refs, norm, dual, x2in, combine, resid, headnorm):
    it = iter(refs)
    x_ref = next(it)
    g_ref = next(it) if norm else None
    w1_ref = next(it)
    w2_ref = next(it) if dual else None
    x2_ref = next(it) if x2in else None
    res_ref = next(it) if resid else None
    gh_ref = next(it) if headnorm else None
    bd_ref = next(it) if headnorm else None
    o_ref = next(it)
    xn_ref = next(it)

    @pl.when(pl.program_id(1) == 0)
    def _():
        xf = x_ref[...].astype(F32)
        if norm:
            ms = jnp.mean(xf * xf, axis=-1, keepdims=True)
            xf = xf * lax.rsqrt(ms + EPS) * g_ref[...]
        xn_ref[...] = xf.astype(BF16)

    xn = xn_ref[...]
    a = jnp.dot(xn, w1_ref[...].astype(BF16), preferred_element_type=F32)
    if dual:
        lhs2 = x2_ref[...].astype(BF16) if x2in else xn
        b = jnp.dot(lhs2, w2_ref[...].astype(BF16), preferred_element_type=F32)
        if combine == "glu":
            y = a * _sigmoid(b)
        elif combine == "swiglu":
            y = a * _sigmoid(a) * b
        else:
            y = _sigmoid(a) * b
    else:
        y = a
    if headnorm:
        y2 = y * y
        hi = y2.astype(BF16)
        lo = (y2 - hi.astype(F32)).astype(BF16)
        bd = bd_ref[...]
        ss = (jnp.dot(hi, bd, preferred_element_type=F32)
              + jnp.dot(lo, bd, preferred_element_type=F32))
        y = y * lax.rsqrt(ss * (1.0 / HEAD_DIM) + EPS) * gh_ref[...]
    if resid:
        y = y + res_ref[...]
    o_ref[...] = y.astype(o_ref.dtype)


def _mm(x, w1, *, g=None, w2=None, w2_col0=0, x2=None, combine=None, res=None,
        g_head=None, n_out=None, out_dtype=F32, tm_cap=512, tn_cap=512):
    M, K = x.shape
    N = n_out if n_out is not None else w1.shape[1]
    tm = min(M, tm_cap)
    tn = _pick_tile(N, tn_cap)
    assert M % tm == 0
    norm, dual, x2in = g is not None, w2 is not None, x2 is not None
    resid, headnorm = res is not None, g_head is not None

    args = [x]
    specs = [pl.BlockSpec((tm, K), lambda i, j: (i, 0))]
    if norm:
        args.append(g.reshape(1, K).astype(F32))
        specs.append(pl.BlockSpec((1, K), lambda i, j: (0, 0)))
    args.append(w1)
    specs.append(pl.BlockSpec((w1.shape[0], tn), lambda i, j: (0, j)))
    if dual:
        assert w2_col0 % tn == 0
        off = w2_col0 // tn
        args.append(w2)
        specs.append(pl.BlockSpec((w2.shape[0], tn), lambda i, j: (0, j + off)))
    if x2in:
        args.append(x2)
        specs.append(pl.BlockSpec((tm, x2.shape[1]), lambda i, j: (i, 0)))
    if resid:
        args.append(res)
        specs.append(pl.BlockSpec((tm, tn), lambda i, j: (i, j)))
    if headnorm:
        gh = jnp.tile(g_head.astype(F32), tn // HEAD_DIM).reshape(1, tn)
        hid = jnp.arange(tn, dtype=jnp.int32) // HEAD_DIM
        bd = (hid[:, None] == hid[None, :]).astype(BF16)
        args += [gh, bd]
        specs += [pl.BlockSpec((1, tn), lambda i, j: (0, 0)),
                  pl.BlockSpec((tn, tn), lambda i, j: (0, 0))]

    body = functools.partial(_mm_body, norm=norm, dual=dual, x2in=x2in, combine=combine,
                             resid=resid, headnorm=headnorm)
    return pl.pallas_call(
        body,
        grid=(M // tm, N // tn),
        in_specs=specs,
        out_specs=pl.BlockSpec((tm, tn), lambda i, j: (i, j)),
        out_shape=jax.ShapeDtypeStruct((M, N), out_dtype),
        scratch_shapes=[pltpu.VMEM((tm, K), BF16)],
        compiler_params=_params(("parallel", "arbitrary")),
        name="fused_mm",
    )(*args)


def _ln_swish(c, lng, lnb):
    mu = jnp.mean(c, axis=-1, keepdims=True)
    xc = c - mu
    var = jnp.mean(xc * xc, axis=-1, keepdims=True)
    y = xc * lax.rsqrt(var + EPS) * lng + lnb
    return y * _sigmoid(y)


HALO = 32
CONV_ROWS = 32


def _conv_body(g_ref, halo_ref, st_ref, wdw_ref, bdw_ref, lng_ref, lnb_ref, o_ref, ext_ref, *, tt):
    i = pl.program_id(1)

    @pl.when(i == 0)
    def _():
        ext_ref[0:HALO, :] = st_ref[0]

    @pl.when(i > 0)
    def _():
        ext_ref[0:HALO, :] = halo_ref[0]

    ext_ref[HALO:HALO + tt, :] = g_ref[0]
    first = HALO - (CONV_W - 1)
    for r in range(tt // CONV_ROWS):
        acc = jnp.zeros((CONV_ROWS, g_ref.shape[-1]), F32)
        for w in range(CONV_W):
            s = r * CONV_ROWS + first + w
            acc = acc + ext_ref[s:s + CONV_ROWS, :] * wdw_ref[w:w + 1, :]
        u = _ln_swish(acc + bdw_ref[...], lng_ref[...], lnb_ref[...])
        o_ref[0, r * CONV_ROWS:(r + 1) * CONV_ROWS, :] = u.astype(o_ref.dtype)


def _conv_prompt(g, state, w_dw, b_dw, ln_g, ln_b, tt=256):
    B, T, D = g.shape
    st = jnp.concatenate([jnp.zeros((B, HALO - (CONV_W - 1), D), F32), state.astype(F32)], axis=1)
    per = tt // HALO
    row = lambda a: a.reshape(1, D).astype(F32)
    const = lambda b, i: (0, 0)
    return pl.pallas_call(
        functools.partial(_conv_body, tt=tt),
        grid=(B, T // tt),
        in_specs=[pl.BlockSpec((1, tt, D), lambda b, i: (b, i, 0)),
                  pl.BlockSpec((1, HALO, D), lambda b, i: (b, jnp.maximum(i * per - 1, 0), 0)),
                  pl.BlockSpec((1, HALO, D), lambda b, i: (b, 0, 0)),
                  pl.BlockSpec((CONV_W, D), const),
                  pl.BlockSpec((1, D), const), pl.BlockSpec((1, D), const), pl.BlockSpec((1, D), const)],
        out_specs=pl.BlockSpec((1, tt, D), lambda b, i: (b, i, 0)),
        out_shape=jax.ShapeDtypeStruct((B, T, D), BF16),
        scratch_shapes=[pltpu.VMEM((HALO + tt, D), F32)],
        compiler_params=_params(("parallel", "arbitrary")),
        name="conv_prompt",
    )(g, g, st, w_dw.astype(F32), row(b_dw), row(ln_g), row(ln_b))


def _conv_step_body(ext_ref, wdw_ref, bdw_ref, lng_ref, lnb_ref, o_ref):
    acc = jnp.zeros(o_ref.shape, F32)
    for w in range(CONV_W):
        acc = acc + ext_ref[w] * wdw_ref[w:w + 1, :]
    o_ref[...] = _ln_swish(acc + bdw_ref[...], lng_ref[...], lnb_ref[...]).astype(o_ref.dtype)


def _conv_step(ext_t, w_dw, b_dw, ln_g, ln_b, rows=32):
    _, B, D = ext_t.shape
    row = lambda a: a.reshape(1, D).astype(F32)
    const = lambda i: (0, 0)
    return pl.pallas_call(
        _conv_step_body,
        grid=(B // rows,),
        in_specs=[pl.BlockSpec((CONV_W, rows, D), lambda i: (0, i, 0)),
                  pl.BlockSpec((CONV_W, D), const),
                  pl.BlockSpec((1, D), const), pl.BlockSpec((1, D), const), pl.BlockSpec((1, D), const)],
        out_specs=pl.BlockSpec((rows, D), lambda i: (i, 0)),
        out_shape=jax.ShapeDtypeStruct((B, D), BF16),
        compiler_params=_params(("parallel",)),
        name="conv_step",
    )(ext_t, w_dw.astype(F32), row(b_dw), row(ln_g), row(ln_b))


def _suffix_matrix():
    j = jnp.arange(CHUNK, dtype=jnp.int32)
    upper = (j[:, None] > j[None, :]).astype(BF16)
    return jnp.concatenate([upper, jnp.ones((CHUNK, CHUNK), BF16)], axis=1)


def _stick_logs(z):
    sp = jnp.maximum(z, 0.0) + jnp.log1p(jnp.exp(-jnp.abs(z)))
    return -sp, z - sp


def _attn_body(bias_ref, q_ref, k_ref, v_ref, u_ref, o_ref, *, tq):
    h = pl.program_id(1)
    i = pl.program_id(2)
    bias = bias_ref[h]
    q = (q_ref[0, 0] * (HEAD_DIM ** -0.5)).astype(BF16)
    u = u_ref[...]
    nchunk = tq // CHUNK
    row = lax.broadcasted_iota(jnp.int32, (tq, CHUNK), 0)
    col = lax.broadcasted_iota(jnp.int32, (tq, CHUNK), 1)

    def block(kb, c, o, masked):
        ks = pl.multiple_of(kb * tq, tq)
        k = k_ref[0, 0, pl.ds(ks, tq), :]
        v = v_ref[0, 0, pl.ds(ks, tq), :]
        z = lax.dot_general(q, k, (((1,), (1,)), ((), ())), preferred_element_type=F32) + bias
        l1m, lbeta = _stick_logs(z)
        ws = [None] * nchunk
        off = c
        for m in reversed(range(nchunk)):
            sl = slice(m * CHUNK, (m + 1) * CHUNK)
            lc = l1m[:, sl]
            if masked:
                keep = (col + m * CHUNK) < row
                lc = jnp.where(keep, lc, 0.0)
            st = jnp.dot(lc.astype(BF16), u, preferred_element_type=F32)
            w = jnp.exp(lbeta[:, sl] + st[:, :CHUNK] + off)
            if masked:
                w = jnp.where(keep, w, 0.0)
            ws[m] = w.astype(BF16)
            off = off + st[:, CHUNK:]
        wfull = jnp.concatenate(ws, axis=1)
        o = o + jnp.dot(wfull, v, preferred_element_type=F32)
        return off, o

    c0 = jnp.zeros((tq, CHUNK), F32)
    o0 = jnp.zeros((tq, HEAD_DIM), F32)
    c1, o1 = block(i, c0, o0, True)

    def loop(jj, carry):
        c, o = carry
        return block(i - 1 - jj, c, o, False)

    _, o_fin = lax.fori_loop(0, i, loop, (c1, o1))
    o_ref[0, 0] = o_fin.astype(o_ref.dtype)


def _attn_prompt(q, k, v, bias, tq=512):
    B, H, T, Dh = q.shape
    return pl.pallas_call(
        functools.partial(_attn_body, tq=tq),
        grid_spec=pltpu.PrefetchScalarGridSpec(
            num_scalar_prefetch=0,
            grid=(B, H, T // tq),
            in_specs=[pl.BlockSpec(memory_space=pltpu.SMEM),
                      pl.BlockSpec((1, 1, tq, Dh), lambda b, h, i: (b, h, i, 0)),
                      pl.BlockSpec((1, 1, T, Dh), lambda b, h, i: (b, h, 0, 0)),
                      pl.BlockSpec((1, 1, T, Dh), lambda b, h, i: (b, h, 0, 0)),
                      pl.BlockSpec((CHUNK, 2 * CHUNK), lambda b, h, i: (0, 0))],
            out_specs=pl.BlockSpec((1, 1, tq, Dh), lambda b, h, i: (b, h, i, 0)),
        ),
        out_shape=jax.ShapeDtypeStruct((B, H, T, Dh), BF16),
        compiler_params=_params(("parallel", "parallel", "arbitrary")),
        name="stick_attn_prompt",
    )(bias.astype(F32), q, k, v, _suffix_matrix())


def _decode_body(pt_ref, q_ref, k_ref, v_ref, bias_ref, hm_ref, u_ref, o_ref, c_ref, acc_ref):
    p = pl.program_id(1)

    @pl.when(p == 0)
    def _():
        c_ref[...] = jnp.zeros_like(c_ref)
        acc_ref[...] = jnp.zeros_like(acc_ref)

    hm = hm_ref[...]
    qbd = (q_ref[0] * (HEAD_DIM ** -0.5) * hm).astype(BF16)
    kp = k_ref[0].astype(BF16)
    vp = v_ref[0].astype(BF16)
    z = lax.dot_general(qbd, kp, (((1,), (1,)), ((), ())), preferred_element_type=F32) + bias_ref[...]
    l1m, lbeta = _stick_logs(z)
    st = jnp.dot(l1m.astype(BF16), u_ref[...], preferred_element_type=F32)
    w = jnp.exp(lbeta + st[:, :CHUNK] + c_ref[...])
    c_ref[...] += st[:, CHUNK:]
    acc_ref[...] += jnp.dot(w.astype(BF16), vp, preferred_element_type=F32)

    @pl.when(p == pl.num_programs(1) - 1)
    def _():
        o_ref[0] = jnp.sum(acc_ref[...] * hm, axis=0, keepdims=True).astype(o_ref.dtype)


def _attn_decode(q, cache_k, cache_v, page_table, bias):
    S, D = q.shape
    n_pages = page_table.shape[1]
    ck = cache_k.reshape(cache_k.shape[0], PAGE_SIZE, D)
    cv = cache_v.reshape(cache_v.shape[0], PAGE_SIZE, D)
    hid = jnp.arange(D, dtype=jnp.int32) // HEAD_DIM
    hm = (hid[None, :] == jnp.arange(N_HEADS, dtype=jnp.int32)[:, None]).astype(F32)
    bias_rep = jnp.broadcast_to(bias.astype(F32)[:, None], (N_HEADS, PAGE_SIZE))
    page = lambda s, p, pt: (pt[s, n_pages - 1 - p], 0, 0)
    const = lambda s, p, pt: (0, 0)
    out = pl.pallas_call(
        _decode_body,
        grid_spec=pltpu.PrefetchScalarGridSpec(
            num_scalar_prefetch=1,
            grid=(S, n_pages),
            in_specs=[pl.BlockSpec((1, 1, D), lambda s, p, pt: (s, 0, 0)),
                      pl.BlockSpec((1, PAGE_SIZE, D), page),
                      pl.BlockSpec((1, PAGE_SIZE, D), page),
                      pl.BlockSpec((N_HEADS, PAGE_SIZE), const),
                      pl.BlockSpec((N_HEADS, D), const),
                      pl.BlockSpec((CHUNK, 2 * CHUNK), const)],
            out_specs=pl.BlockSpec((1, 1, D), lambda s, p, pt: (s, 0, 0)),
            scratch_shapes=[pltpu.VMEM((N_HEADS, PAGE_SIZE), F32), pltpu.VMEM((N_HEADS, D), F32)],
        ),
        out_shape=jax.ShapeDtypeStruct((S, 1, D), BF16),
        compiler_params=_params(("parallel", "arbitrary")),
        name="stick_attn_decode",
    )(page_table, q.reshape(S, 1, D), ck, cv, bias_rep, hm, _suffix_matrix())
    return out.reshape(S, D)


def _moe_body(x_ref, g_ref, wr_ref, wg_ref, wu_ref, wd_ref, o_ref, xn_ref, comb_ref, acc_ref):
    e = pl.program_id(1)
    f = pl.program_id(2)
    tm = x_ref.shape[0]
    lane = lax.broadcasted_iota(jnp.int32, (tm, LANES), 1)

    @pl.when((e == 0) & (f == 0))
    def _():
        xf = x_ref[...]
        ms = jnp.mean(xf * xf, axis=-1, keepdims=True)
        hn = xf * lax.rsqrt(ms + EPS) * g_ref[...]
        xn_ref[...] = hn.astype(BF16)
        logits = jnp.dot(hn, wr_ref[...], preferred_element_type=F32,
                         precision=lax.Precision.HIGHEST)
        lanef = lane.astype(F32)
        neg = jnp.float32(-jnp.inf)
        lg = jnp.where(lane < N_EXPERTS, logits, neg)
        m1 = jnp.max(lg, axis=-1, keepdims=True)
        i1 = jnp.min(jnp.where(lg == m1, lanef, float(LANES)), axis=-1, keepdims=True)
        lg2 = jnp.where(lanef == i1, neg, lg)
        m2 = jnp.max(lg2, axis=-1, keepdims=True)
        i2 = jnp.min(jnp.where(lg2 == m2, lanef, float(LANES)), axis=-1, keepdims=True)
        e2 = jnp.exp(m2 - m1)
        g1 = 1.0 / (1.0 + e2)
        g2 = e2 / (1.0 + e2)
        comb_ref[...] = jnp.where(lanef == i1, g1, jnp.where(lanef == i2, g2, 0.0))
        acc_ref[...] = jnp.zeros_like(acc_ref)

    xn = xn_ref[...]
    a = jnp.dot(xn, wg_ref[0].astype(BF16), preferred_element_type=F32)
    b = jnp.dot(xn, wu_ref[0].astype(BF16), preferred_element_type=F32)
    cw = jnp.sum(jnp.where(lane == e, comb_ref[...], 0.0), axis=-1, keepdims=True)
    t = (a * _sigmoid(a) * b * cw).astype(BF16)
    acc_ref[...] += jnp.dot(t, wd_ref[0].astype(BF16), preferred_element_type=F32)

    @pl.when((e == pl.num_programs(1) - 1) & (f == pl.num_programs(2) - 1))
    def _():
        o_ref[...] = x_ref[...] + acc_ref[...]


def _moe(x, g, w_router, w_g, w_u, w_d, tm_cap=1024, tf_cap=256):
    M, D = x.shape
    E, _, F = w_g.shape
    tm = min(M, tm_cap)
    tf = _pick_tile(F, tf_cap)
    wr = jnp.zeros((D, LANES), F32).at[:, :E].set(w_router.astype(F32))
    return pl.pallas_call(
        _moe_body,
        grid=(M // tm, E, F // tf),
        in_specs=[pl.BlockSpec((tm, D), lambda i, e, f: (i, 0)),
                  pl.BlockSpec((1, D), lambda i, e, f: (0, 0)),
                  pl.BlockSpec((D, LANES), lambda i, e, f: (0, 0)),
                  pl.BlockSpec((1, D, tf), lambda i, e, f: (e, 0, f)),
                  pl.BlockSpec((1, D, tf), lambda i, e, f: (e, 0, f)),
                  pl.BlockSpec((1, tf, D), lambda i, e, f: (e, f, 0))],
        out_specs=pl.BlockSpec((tm, D), lambda i, e, f: (i, 0)),
        out_shape=jax.ShapeDtypeStruct((M, D), F32),
        scratch_shapes=[pltpu.VMEM((tm, D), BF16), pltpu.VMEM((tm, LANES), F32),
                        pltpu.VMEM((tm, D), F32)],
        compiler_params=_params(("parallel", "arbitrary", "arbitrary")),
        name="moe_top2",
    )(x, g.reshape(1, D).astype(F32), wr, w_g, w_u, w_d)


def _trunk(x3d, p, W, conv_fn, attn_fn):
    B, T, D = x3d.shape
    M = B * T
    x = x3d.reshape(M, D)
    p = p.reshape(p.shape[0], M, p.shape[-1])

    glu = _mm(x, W["w_pw1"][0], g=W["g_mix"][0], w2=W["w_pw1"][0], w2_col0=D, combine="glu", n_out=D)
    u, new_state = conv_fn(glu.reshape(B, T, D))
    x = _mm(u.reshape(M, D), W["w_pw2"][0], res=x)
    hid = _mm(x, W["w_ff_gate"][0], g=W["g_ffn"][0], w2=W["w_ff_up"][0], combine="swiglu",
              out_dtype=BF16)
    x = _mm(hid, W["w_ff_down"][0], res=x)
    x = _mm(x, W["w_pe_gate"][0], g=W["g_pe"][0], w2=W["w_pe_proj"][0], x2=p[0], combine="pe", res=x)

    k_new = _mm(x, W["w_k"], g=W["g_kv"], g_head=W["g_k"])
    v_new = _mm(x, W["w_v"], g=W["g_kv"])

    q = _mm(x, W["w_q"][0], g=W["g_mix"][1], g_head=W["g_q"][0], out_dtype=BF16)
    o = attn_fn(q, k_new, v_new)
    x = _mm(o, W["w_o"][0], res=x)
    x = _moe(x, W["g_ffn"][1], W["w_router"][0], W["w_ex_gate"][0], W["w_ex_up"][0], W["w_ex_down"][0])
    x = _mm(x, W["w_pe_gate"][1], g=W["g_pe"][1], w2=W["w_pe_proj"][1], x2=p[1], combine="pe", res=x)

    heads = lambda t: t.reshape(B, T, N_HEADS, HEAD_DIM)
    return x.reshape(B, T, D), new_state, heads(k_new), heads(v_new)


def kernel(x_prompt, x_sample, state_conv, cache_k, cache_v, page_table, p_prompt, p_sample, g_mix, w_pw1, w_dw, b_dw, ln_g, ln_b, w_pw2, g_kv, w_k, w_v, g_k, w_q, g_q, b_sb, w_o, g_ffn, w_ff_gate, w_ff_up, w_ff_down, w_router, w_ex_gate, w_ex_up, w_ex_down, g_pe, w_pe_gate, w_pe_proj):
    W = dict(g_mix=g_mix, w_pw1=w_pw1, w_pw2=w_pw2, g_kv=g_kv, w_k=w_k, w_v=w_v, g_k=g_k, w_q=w_q,
             g_q=g_q, w_o=w_o, g_ffn=g_ffn, w_ff_gate=w_ff_gate, w_ff_up=w_ff_up,
             w_ff_down=w_ff_down, w_router=w_router, w_ex_gate=w_ex_gate, w_ex_up=w_ex_up,
             w_ex_down=w_ex_down, g_pe=g_pe, w_pe_gate=w_pe_gate, w_pe_proj=w_pe_proj)
    B, T, D = x_prompt.shape
    S = x_sample.shape[0]
    hist = CONV_W - 1

    def conv_prompt(glu):
        zero = jnp.zeros((B, hist, D), F32)
        u = _conv_prompt(glu, zero, w_dw[0], b_dw[0], ln_g[0], ln_b[0])
        return u, glu[:, T - hist:][None]

    def attn_prompt(q, k, v):
        hm = lambda t: t.astype(BF16).reshape(B, T, N_HEADS, HEAD_DIM).transpose(0, 2, 1, 3)
        o = _attn_prompt(hm(q), hm(k), hm(v), b_sb[0])
        return o.transpose(0, 2, 1, 3).reshape(B * T, D)

    y_p, conv_p, k_p, v_p = _trunk(x_prompt, p_prompt, W, conv_prompt, attn_prompt)

    def conv_sample(glu):
        ext = jnp.concatenate([state_conv[0], glu], axis=1)
        u = _conv_step(ext.transpose(1, 0, 2), w_dw[0], b_dw[0], ln_g[0], ln_b[0])
        return u.reshape(S, 1, D), ext[:, 1:][None]

    def attn_sample(q, k, v):
        return _attn_decode(q, cache_k, cache_v, page_table, b_sb[0])

    y_s, conv_s, k_s, v_s = _trunk(x_sample, p_sample, W, conv_sample, attn_sample)
    return (y_p, y_s, conv_p, conv_s, k_p, v_p, k_s, v_s)
```

```python
import functools

import jax
import jax.numpy as jnp
from jax import lax
from jax.experimental import pallas as pl
from jax.experimental.pallas import tpu as pltpu

F32 = jnp.float32
BF16 = jnp.bfloat16
EPS = 1e-6
HEAD_DIM = 64
N_HEADS = 16
CONV_W = 31
N_EXPERTS = 8
PAGE_SIZE = 128
LANES = 128
VMEM_LIMIT_BYTES = 56 * 1024 * 1024
CHUNK = 128


def _sigmoid(x):
    return 1.0 / (1.0 + jnp.exp(-x))


def _pick_tile(n, cap):
    best = None
    t = LANES
    while t <= min(n, cap):
        if n % t == 0:
            best = t
        t += LANES
    assert best is not None, n
    return best


def _params(sem):
    return pltpu.CompilerParams(dimension_semantics=sem, vmem_limit_bytes=VMEM_LIMIT_BYTES)


def _mm_body(*refs, norm, dual, x2in, combine, resid, headnorm):
    it = iter(refs)
    x_ref = next(it)
    g_ref = next(it) if norm else None
    w1_ref = next(it)
    w2_ref = next(it) if dual else None
    x2_ref = next(it) if x2in else None
    res_ref = next(it) if resid else None
    gh_ref = next(it) if headnorm else None
    bd_ref = next(it) if headnorm else None
    o_ref = next(it)
    xn_ref = next(it)

    @pl.when(pl.program_id(1) == 0)
    def _():
        xf = x_ref[...].astype(F32)
        if norm:
            ms = jnp.mean(xf * xf, axis=-1, keepdims=True)
            xf = xf * lax.rsqrt(ms + EPS) * g_ref[...]
        xn_ref[...] = xf.astype(BF16)

    xn = xn_ref[...]
    a = jnp.dot(xn, w1_ref[...].astype(BF16), preferred_element_type=F32)
    if dual:
        lhs2 = x2_ref[...].astype(BF16) if x2in else xn
        b = jnp.dot(lhs2, w2_ref[...].astype(BF16), preferred_element_type=F32)
        if combine == "glu":
            y = a * _sigmoid(b)
        elif combine == "swiglu":
            y = a * _sigmoid(a) * b
        else:
            y = _sigmoid(a) * b
    else:
        y = a
    if headnorm:
        y2 = y * y
        hi = y2.astype(BF16)
        lo = (y2 - hi.astype(F32)).astype(BF16)
        bd = bd_ref[...]
        ss = (jnp.dot(hi, bd, preferred_element_type=F32)
              + jnp.dot(lo, bd, preferred_element_type=F32))
        y = y * lax.rsqrt(ss * (1.0 / HEAD_DIM) + EPS) * gh_ref[...]
    if resid:
        y = y + res_ref[...]
    o_ref[...] = y.astype(o_ref.dtype)


def _mm(x, w1, *, g=None, w2=None, w2_col0=0, x2=None, combine=None, res=None,
        g_head=None, n_out=None, out_dtype=F32, tm_cap=512, tn_cap=512):
    M, K = x.shape
    N = n_out if n_out is not None else w1.shape[1]
    tm = min(M, tm_cap)
    tn = _pick_tile(N, tn_cap)
    assert M % tm == 0
    norm, dual, x2in = g is not None, w2 is not None, x2 is not None
    resid, headnorm = res is not None, g_head is not None

    args = [x]
    specs = [pl.BlockSpec((tm, K), lambda i, j: (i, 0))]
    if norm:
        args.append(g.reshape(1, K).astype(F32))
        specs.append(pl.BlockSpec((1, K), lambda i, j: (0, 0)))
    args.append(w1)
    specs.append(pl.BlockSpec((w1.shape[0], tn), lambda i, j: (0, j)))
    if dual:
        assert w2_col0 % tn == 0
        off = w2_col0 // tn
        args.append(w2)
        specs.append(pl.BlockSpec((w2.shape[0], tn), lambda i, j: (0, j + off)))
    if x2in:
        args.append(x2)
        specs.append(pl.BlockSpec((tm, x2.shape[1]), lambda i, j: (i, 0)))
    if resid:
        args.append(res)
        specs.append(pl.BlockSpec((tm, tn), lambda i, j: (i, j)))
    if headnorm:
        gh = jnp.tile(g_head.astype(F32), tn // HEAD_DIM).reshape(1, tn)
        hid = jnp.arange(tn, dtype=jnp.int32) // HEAD_DIM
        bd = (hid[:, None] == hid[None, :]).astype(BF16)
        args += [gh, bd]
        specs += [pl.BlockSpec((1, tn), lambda i, j: (0, 0)),
                  pl.BlockSpec((tn, tn), lambda i, j: (0, 0))]

    body = functools.partial(_mm_body, norm=norm, dual=dual, x2in=x2in, combine=combine,
                             resid=resid, headnorm=headnorm)
    return pl.pallas_call(
        body,
        grid=(M // tm, N // tn),
        in_specs=specs,
        out_specs=pl.BlockSpec((tm, tn), lambda i, j: (i, j)),
        out_shape=jax.ShapeDtypeStruct((M, N), out_dtype),
        scratch_shapes=[pltpu.VMEM((tm, K), BF16)],
        compiler_params=_params(("parallel", "arbitrary")),
        name="fused_mm",
    )(*args)


def _ln_swish(c, lng, lnb):
    mu = jnp.mean(c, axis=-1, keepdims=True)
    xc = c - mu
    var = jnp.mean(xc * xc, axis=-1, keepdims=True)
    y = xc * lax.rsqrt(var + EPS) * lng + lnb
    return y * _sigmoid(y)


HALO = 32
CONV_ROWS = 32


def _conv_body(g_ref, halo_ref, st_ref, wdw_ref, bdw_ref, lng_ref, lnb_ref, o_ref, ext_ref, *, tt):
    i = pl.program_id(1)

    @pl.when(i == 0)
    def _():
        ext_ref[0:HALO, :] = st_ref[0]

    @pl.when(i > 0)
    def _():
        ext_ref[0:HALO, :] = halo_ref[0]

    ext_ref[HALO:HALO + tt, :] = g_ref[0]
    first = HALO - (CONV_W - 1)
    for r in range(tt // CONV_ROWS):
        acc = jnp.zeros((CONV_ROWS, g_ref.shape[-1]), F32)
        for w in range(CONV_W):
            s = r * CONV_ROWS + first + w
            acc = acc + ext_ref[s:s + CONV_ROWS, :] * wdw_ref[w:w + 1, :]
        u = _ln_swish(acc + bdw_ref[...], lng_ref[...], lnb_ref[...])
        o_ref[0, r * CONV_ROWS:(r + 1) * CONV_ROWS, :] = u.astype(o_ref.dtype)


def _conv_prompt(g, state, w_dw, b_dw, ln_g, ln_b, tt=256):
    B, T, D = g.shape
    st = jnp.concatenate([jnp.zeros((B, HALO - (CONV_W - 1), D), F32), state.astype(F32)], axis=1)
    per = tt // HALO
    row = lambda a: a.reshape(1, D).astype(F32)
    const = lambda b, i: (0, 0)
    return pl.pallas_call(
        functools.partial(_conv_body, tt=tt),
        grid=(B, T // tt),
        in_specs=[pl.BlockSpec((1, tt, D), lambda b, i: (b, i, 0)),
                  pl.BlockSpec((1, HALO, D), lambda b, i: (b, jnp.maximum(i * per - 1, 0), 0)),
                  pl.BlockSpec((1, HALO, D), lambda b, i: (b, 0, 0)),
                  pl.BlockSpec((CONV_W, D), const),
                  pl.BlockSpec((1, D), const), pl.BlockSpec((1, D), const), pl.BlockSpec((1, D), const)],
        out_specs=pl.BlockSpec((1, tt, D), lambda b, i: (b, i, 0)),
        out_shape=jax.ShapeDtypeStruct((B, T, D), BF16),
        scratch_shapes=[pltpu.VMEM((HALO + tt, D), F32)],
        compiler_params=_params(("parallel", "arbitrary")),
        name="conv_prompt",
    )(g, g, st, w_dw.astype(F32), row(b_dw), row(ln_g), row(ln_b))


def _conv_step_body(ext_ref, wdw_ref, bdw_ref, lng_ref, lnb_ref, o_ref):
    acc = jnp.zeros(o_ref.shape, F32)
    for w in range(CONV_W):
        acc = acc + ext_ref[w] * wdw_ref[w:w + 1, :]
    o_ref[...] = _ln_swish(acc + bdw_ref[...], lng_ref[...], lnb_ref[...]).astype(o_ref.dtype)


def _conv_step(ext_t, w_dw, b_dw, ln_g, ln_b, rows=32):
    _, B, D = ext_t.shape
    row = lambda a: a.reshape(1, D).astype(F32)
    const = lambda i: (0, 0)
    return pl.pallas_call(
        _conv_step_body,
        grid=(B // rows,),
        in_specs=[pl.BlockSpec((CONV_W, rows, D), lambda i: (0, i, 0)),
                  pl.BlockSpec((CONV_W, D), const),
                  pl.BlockSpec((1, D), const), pl.BlockSpec((1, D), const), pl.BlockSpec((1, D), const)],
        out_specs=pl.BlockSpec((rows, D), lambda i: (i, 0)),
        out_shape=jax.ShapeDtypeStruct((B, D), BF16),
        compiler_params=_params(("parallel",)),
        name="conv_step",
    )(ext_t, w_dw.astype(F32), row(b_dw), row(ln_g), row(ln_b))


def _suffix_matrix():
    j = jnp.arange(CHUNK, dtype=jnp.int32)
    upper = (j[:, None] > j[None, :]).astype(BF16)
    return jnp.concatenate([upper, jnp.ones((CHUNK, CHUNK), BF16)], axis=1)


def _stick_logs(z):
    sp = jnp.maximum(z, 0.0) + jnp.log1p(jnp.exp(-jnp.abs(z)))
    return -sp, z - sp


def _attn_body(bias_ref, q_ref, k_ref, v_ref, u_ref, o_ref, *, tq):
    h = pl.program_id(1)
    i = pl.program_id(2)
    bias = bias_ref[h]
    q = (q_ref[0, 0] * (HEAD_DIM ** -0.5)).astype(BF16)
    u = u_ref[...]
    nchunk = tq // CHUNK
    row = lax.broadcasted_iota(jnp.int32, (tq, CHUNK), 0)
    col = lax.broadcasted_iota(jnp.int32, (tq, CHUNK), 1)

    def block(kb, c, o, masked):
        ks = pl.multiple_of(kb * tq, tq)
        k = k_ref[0, 0, pl.ds(ks, tq), :]
        v = v_ref[0, 0, pl.ds(ks, tq), :]
        z = lax.dot_general(q, k, (((1,), (1,)), ((), ())), preferred_element_type=F32) + bias
        l1m, lbeta = _stick_logs(z)
        ws = [None] * nchunk
        off = c
        for m in reversed(range(nchunk)):
            sl = slice(m * CHUNK, (m + 1) * CHUNK)
            lc = l1m[:, sl]
            if masked:
                keep = (col + m * CHUNK) < row
                lc = jnp.where(keep, lc, 0.0)
            st = jnp.dot(lc.astype(BF16), u, preferred_element_type=F32)
            w = jnp.exp(lbeta[:, sl] + st[:, :CHUNK] + off)
            if masked:
                w = jnp.where(keep, w, 0.0)
            ws[m] = w.astype(BF16)
            off = off + st[:, CHUNK:]
        wfull = jnp.concatenate(ws, axis=1)
        o = o + jnp.dot(wfull, v, preferred_element_type=F32)
        return off, o

    c0 = jnp.zeros((tq, CHUNK), F32)
    o0 = jnp.zeros((tq, HEAD_DIM), F32)
    c1, o1 = block(i, c0, o0, True)

    def loop(jj, carry):
        c, o = carry
        return block(i - 1 - jj, c, o, False)

    _, o_fin = lax.fori_loop(0, i, loop, (c1, o1))
    o_ref[0, 0] = o_fin.astype(o_ref.dtype)


def _attn_prompt(q, k, v, bias, tq=512):
    B, H, T, Dh = q.shape
    return pl.pallas_call(
        functools.partial(_attn_body, tq=tq),
        grid_spec=pltpu.PrefetchScalarGridSpec(
            num_scalar_prefetch=0,
            grid=(B, H, T // tq),
            in_specs=[pl.BlockSpec(memory_space=pltpu.SMEM),
                      pl.BlockSpec((1, 1, tq, Dh), lambda b, h, i: (b, h, i, 0)),
                      pl.BlockSpec((1, 1, T, Dh), lambda b, h, i: (b, h, 0, 0)),
                      pl.BlockSpec((1, 1, T, Dh), lambda b, h, i: (b, h, 0, 0)),
                      pl.BlockSpec((CHUNK, 2 * CHUNK), lambda b, h, i: (0, 0))],
            out_specs=pl.BlockSpec((1, 1, tq, Dh), lambda b, h, i: (b, h, i, 0)),
        ),
        out_shape=jax.ShapeDtypeStruct((B, H, T, Dh), BF16),
        compiler_params=_params(("parallel", "parallel", "arbitrary")),
        name="stick_attn_prompt",
    )(bias.astype(F32), q, k, v, _suffix_matrix())


def _decode_body(pt_ref, q_ref, k_ref, v_ref, bias_ref, u_ref, o_ref, c_ref, acc_ref):
    p = pl.program_id(1)
    cols = PAGE_SIZE * N_HEADS

    @pl.when(p == 0)
    def _():
        c_ref[...] = jnp.zeros_like(c_ref)
        acc_ref[...] = jnp.zeros_like(acc_ref)

    q = (q_ref[0] * (HEAD_DIM ** -0.5)).astype(BF16)
    k2 = k_ref[0].reshape(cols, HEAD_DIM).astype(BF16)
    v2 = v_ref[0].reshape(cols, HEAD_DIM).astype(BF16)
    z = lax.dot_general(q, k2, (((1,), (1,)), ((), ())), preferred_element_type=F32) + bias_ref[...]
    l1m, lbeta = _stick_logs(z)
    row = lax.broadcasted_iota(jnp.int32, (N_HEADS, CHUNK), 0)
    col = lax.broadcasted_iota(jnp.int32, (N_HEADS, CHUNK), 1)
    own = (col & (N_HEADS - 1)) == row
    u = u_ref[...]
    off = c_ref[...]
    ws = [None] * (cols // CHUNK)
    for m in reversed(range(cols // CHUNK)):
        sl = slice(m * CHUNK, (m + 1) * CHUNK)
        lc = jnp.where(own, l1m[:, sl], 0.0)
        st = jnp.dot(lc.astype(BF16), u, preferred_element_type=F32)
        w = jnp.where(own, jnp.exp(lbeta[:, sl] + st[:, :CHUNK] + off), 0.0)
        ws[m] = w.astype(BF16)
        off = off + st[:, CHUNK:]
    c_ref[...] = off
    acc_ref[...] += jnp.dot(jnp.concatenate(ws, axis=1), v2, preferred_element_type=F32)

    @pl.when(p == pl.num_programs(1) - 1)
    def _():
        o_ref[0] = acc_ref[...].astype(o_ref.dtype)


def _attn_decode(q, cache_k, cache_v, page_table, bias):
    S, D = q.shape
    n_pages = page_table.shape[1]
    assert N_HEADS & (N_HEADS - 1) == 0
    bias_rep = jnp.broadcast_to(bias.astype(F32)[:, None], (N_HEADS, PAGE_SIZE * N_HEADS))
    page = lambda s, p, pt: (pt[s, n_pages - 1 - p], 0, 0, 0)
    const = lambda s, p, pt: (0, 0)
    out = pl.pallas_call(
        _decode_body,
        grid_spec=pltpu.PrefetchScalarGridSpec(
            num_scalar_prefetch=1,
            grid=(S, n_pages),
            in_specs=[pl.BlockSpec((1, N_HEADS, HEAD_DIM), lambda s, p, pt: (s, 0, 0)),
                      pl.BlockSpec((1, PAGE_SIZE, N_HEADS, HEAD_DIM), page),
                      pl.BlockSpec((1, PAGE_SIZE, N_HEADS, HEAD_DIM), page),
                      pl.BlockSpec((N_HEADS, PAGE_SIZE * N_HEADS), const),
                      pl.BlockSpec((CHUNK, 2 * CHUNK), const)],
            out_specs=pl.BlockSpec((1, N_HEADS, HEAD_DIM), lambda s, p, pt: (s, 0, 0)),
            scratch_shapes=[pltpu.VMEM((N_HEADS, CHUNK), F32), pltpu.VMEM((N_HEADS, HEAD_DIM), F32)],
        ),
        out_shape=jax.ShapeDtypeStruct((S, N_HEADS, HEAD_DIM), BF16),
        compiler_params=_params(("parallel", "arbitrary")),
        name="stick_attn_decode",
    )(page_table, q.reshape(S, N_HEADS, HEAD_DIM), cache_k, cache_v, bias_rep, _suffix_matrix())
    return out.reshape(S, D)


def _moe_body(x_ref, g_ref, wr_ref, wg_ref, wu_ref, wd_ref, o_ref, xn_ref, comb_ref, acc_ref):
    e = pl.program_id(1)
    f = pl.program_id(2)
    tm = x_ref.shape[0]
    lane = lax.broadcasted_iota(jnp.int32, (tm, LANES), 1)

    @pl.when((e == 0) & (f == 0))
    def _():
        xf = x_ref[...]
        ms = jnp.mean(xf * xf, axis=-1, keepdims=True)
        hn = xf * lax.rsqrt(ms + EPS) * g_ref[...]
        xn_ref[...] = hn.astype(BF16)
        logits = jnp.dot(hn, wr_ref[...], preferred_element_type=F32,
                         precision=lax.Precision.HIGHEST)
        lanef = lane.astype(F32)
        neg = jnp.float32(-jnp.inf)
        lg = jnp.where(lane < N_EXPERTS, logits, neg)
        m1 = jnp.max(lg, axis=-1, keepdims=True)
        i1 = jnp.min(jnp.where(lg == m1, lanef, float(LANES)), axis=-1, keepdims=True)
        lg2 = jnp.where(lanef == i1, neg, lg)
        m2 = jnp.max(lg2, axis=-1, keepdims=True)
        i2 = jnp.min(jnp.where(lg2 == m2, lanef, float(LANES)), axis=-1, keepdims=True)
        e2 = jnp.exp(m2 - m1)
        g1 = 1.0 / (1.0 + e2)
        g2 = e2 / (1.0 + e2)
        comb_ref[...] = jnp.where(lanef == i1, g1, jnp.where(lanef == i2, g2, 0.0))
        acc_ref[...] = jnp.zeros_like(acc_ref)

    xn = xn_ref[...]
    a = jnp.dot(xn, wg_ref[0].astype(BF16), preferred_element_type=F32)
    b = jnp.dot(xn, wu_ref[0].astype(BF16), preferred_element_type=F32)
    cw = jnp.sum(jnp.where(lane == e, comb_ref[...], 0.0), axis=-1, keepdims=True)
    t = (a * _sigmoid(a) * b * cw).astype(BF16)
    acc_ref[...] += jnp.dot(t, wd_ref[0].astype(BF16), preferred_element_type=F32)

    @pl.when((e == pl.num_programs(1) - 1) & (f == pl.num_programs(2) - 1))
    def _():
        o_ref[...] = x_ref[...] + acc_ref[...]


def _moe(x, g, w_router, w_g, w_u, w_d, tm_cap=1024, tf_cap=256):
    M, D = x.shape
    E, _, F = w_g.shape
    tm = min(M, tm_cap)
    tf = _pick_tile(F, tf_cap)
    wr = jnp.zeros((D, LANES), F32).at[:, :E].set(w_router.astype(F32))
    return pl.pallas_call(
        _moe_body,
        grid=(M // tm, E, F // tf),
        in_specs=[pl.BlockSpec((tm, D), lambda i, e, f: (i, 0)),
                  pl.BlockSpec((1, D), lambda i, e, f: (0, 0)),
                  pl.BlockSpec((D, LANES), lambda i, e, f: (0, 0)),
                  pl.BlockSpec((1, D, tf), lambda i, e, f: (e, 0, f)),
                  pl.BlockSpec((1, D, tf), lambda i, e, f: (e, 0, f)),
                  pl.BlockSpec((1, tf, D), lambda i, e, f: (e, f, 0))],
        out_specs=pl.BlockSpec((tm, D), lambda i, e, f: (i, 0)),
        out_shape=jax.ShapeDtypeStruct((M, D), F32),
        scratch_shapes=[pltpu.VMEM((tm, D), BF16), pltpu.VMEM((tm, LANES), F32),
                        pltpu.VMEM((tm, D), F32)],
        compiler_params=_params(("parallel", "arbitrary", "arbitrary")),
        name="moe_top2",
    )(x, g.reshape(1, D).astype(F32), wr, w_g, w_u, w_d)


def _trunk(x3d, p, W, conv_fn, attn_fn):
    B, T, D = x3d.shape
    M = B * T
    x = x3d.reshape(M, D)
    p = p.reshape(p.shape[0], M, p.shape[-1])

    glu = _mm(x, W["w_pw1"][0], g=W["g_mix"][0], w2=W["w_pw1"][0], w2_col0=D, combine="glu", n_out=D)
    u, new_state = conv_fn(glu.reshape(B, T, D))
    x = _mm(u.reshape(M, D), W["w_pw2"][0], res=x)
    hid = _mm(x, W["w_ff_gate"][0], g=W["g_ffn"][0], w2=W["w_ff_up"][0], combine="swiglu",
              out_dtype=BF16)
    x = _mm(hid, W["w_ff_down"][0], res=x)
    x = _mm(x, W["w_pe_gate"][0], g=W["g_pe"][0], w2=W["w_pe_proj"][0], x2=p[0], combine="pe", res=x)

    k_new = _mm(x, W["w_k"], g=W["g_kv"], g_head=W["g_k"])
    v_new = _mm(x, W["w_v"], g=W["g_kv"])

    q = _mm(x, W["w_q"][0], g=W["g_mix"][1], g_head=W["g_q"][0], out_dtype=BF16)
    o = attn_fn(q, k_new, v_new)
    x = _mm(o, W["w_o"][0], res=x)
    x = _moe(x, W["g_ffn"][1], W["w_router"][0], W["w_ex_gate"][0], W["w_ex_up"][0], W["w_ex_down"][0])
    x = _mm(x, W["w_pe_gate"][1], g=W["g_pe"][1], w2=W["w_pe_proj"][1], x2=p[1], combine="pe", res=x)

    heads = lambda t: t.reshape(B, T, N_HEADS, HEAD_DIM)
    return x.reshape(B, T, D), new_state, heads(k_new), heads(v_new)


def kernel(x_prompt, x_sample, state_conv, cache_k, cache_v, page_table, p_prompt, p_sample, g_mix, w_pw1, w_dw, b_dw, ln_g, ln_b, w_pw2, g_kv, w_k, w_v, g_k, w_q, g_q, b_sb, w_o, g_ffn, w_ff_gate, w_ff_up, w_ff_down, w_router, w_ex_gate, w_ex_up, w_ex_down, g_pe, w_pe_gate, w_pe_proj):
    W = dict(g_mix=g_mix, w_pw1=w_pw1, w_pw2=w_pw2, g_kv=g_kv, w_k=w_k, w_v=w_v, g_k=g_k, w_q=w_q,
             g_q=g_q, w_o=w_o, g_ffn=g_ffn, w_ff_gate=w_ff_gate, w_ff_up=w_ff_up,
             w_ff_down=w_ff_down, w_router=w_router, w_ex_gate=w_ex_gate, w_ex_up=w_ex_up,
             w_ex_down=w_ex_down, g_pe=g_pe, w_pe_gate=w_pe_gate, w_pe_proj=w_pe_proj)
    B, T, D = x_prompt.shape
    S = x_sample.shape[0]
    hist = CONV_W - 1

    def conv_prompt(glu):
        zero = jnp.zeros((B, hist, D), F32)
        u = _conv_prompt(glu, zero, w_dw[0], b_dw[0], ln_g[0], ln_b[0])
        return u, glu[:, T - hist:][None]

    def attn_prompt(q, k, v):
        hm = lambda t: t.astype(BF16).reshape(B, T, N_HEADS, HEAD_DIM).transpose(0, 2, 1, 3)
        o = _attn_prompt(hm(q), hm(k), hm(v), b_sb[0])
        return o.transpose(0, 2, 1, 3).reshape(B * T, D)

    y_p, conv_p, k_p, v_p = _trunk(x_prompt, p_prompt, W, conv_prompt, attn_prompt)

    def conv_sample(glu):
        ext = jnp.concatenate([state_conv[0], glu], axis=1)
        u = _conv_step(ext.transpose(1, 0, 2), w_dw[0], b_dw[0], ln_g[0], ln_b[0])
        return u.reshape(S, 1, D), ext[:, 1:][None]

    def attn_sample(q, k, v):
        return _attn_decode(q, cache_k, cache_v, page_table, b_sb[0])

    y_s, conv_s, k_s, v_s = _trunk(x_sample, p_sample, W, conv_sample, attn_sample)
    return (y_p, y_s, conv_p, conv_s, k_p, v_p, k_s, v_s)
```
